```python
import jax, jax.numpy as jnp
from jax import lax
import numpy as np

D_MODEL = 1024
BATCH = 2
SEQ = 16384
DEPTH = 1
DEC_BATCH = 16
DEC_SEQ = 4096
PAST_LEN = 128

MIX_WIDTH = D_MODEL
POOL_WIDTH = MIX_WIDTH // 2
CONV_WIDTH = MIX_WIDTH - POOL_WIDTH
POOL_WINDOWS = (2, 4, 8, 16)
N_POOL_GROUPS = len(POOL_WINDOWS)
POOL_GROUP_DIM = POOL_WIDTH // N_POOL_GROUPS
CONV_KERNEL = 31
D_FF = 2816
IN_COLS = POOL_WIDTH + 2 * CONV_WIDTH
EPS = 1e-6

kernel_name = "hybrid_pool_conv_macaron_encoder"


def rmsnorm(x, g):
    xf = x.astype(jnp.float32)
    y = xf * lax.rsqrt(jnp.mean(xf * xf, axis=-1, keepdims=True) + EPS)
    return (y * g.astype(jnp.float32)).astype(x.dtype)


def layernorm(x, g, b):
    xf = x.astype(jnp.float32)
    mu = jnp.mean(xf, axis=-1, keepdims=True)
    var = jnp.mean(jnp.square(xf - mu), axis=-1, keepdims=True)
    y = (xf - mu) * lax.rsqrt(var + EPS)
    return (y * g.astype(jnp.float32) + b.astype(jnp.float32)).astype(x.dtype)


def swiglu_ffn(h, w_gu, w_down):
    gu = h @ w_gu
    g, u = jnp.split(gu, 2, axis=-1)
    return (jax.nn.silu(g) * u) @ w_down


def centred_mean_minus_self(xg, window):
    L = xg.shape[1]
    half = window // 2
    xf = xg.astype(jnp.float32)
    cs = jnp.concatenate([jnp.zeros_like(xf[:, :1]), jnp.cumsum(xf, axis=1)], axis=1)
    t = np.arange(L)
    lo = np.clip(t - half, 0, L)
    hi = np.clip(t + half, 0, L)
    cnt = jnp.asarray((hi - lo).astype(np.float32))[None, :, None]
    win_sum = jnp.take(cs, jnp.asarray(hi), axis=1) - jnp.take(cs, jnp.asarray(lo), axis=1)
    return (win_sum / cnt - xf).astype(xg.dtype)


def pool_mixer(u_pool, pool_w, pool_b, pool_scale):
    B, L, _ = u_pool.shape
    groups = jnp.split(u_pool, N_POOL_GROUPS, axis=-1)
    pooled = jnp.stack([centred_mean_minus_self(g, w) for g, w in zip(groups, POOL_WINDOWS)], axis=2)
    mixed = jnp.einsum("blgc,gcd->blgd", pooled, pool_w) + pool_b
    return mixed.reshape(B, L, POOL_WIDTH) * pool_scale


def conv_mixer(u_val, u_gate, dw_w, dw_b, conv_ln_g, conv_ln_b):
    v = u_val * jax.nn.sigmoid(u_gate)
    y = lax.conv_general_dilated(
        v, dw_w[:, None, :],
        window_strides=(1,),
        padding=[(CONV_KERNEL // 2, CONV_KERNEL // 2)],
        dimension_numbers=("NWC", "WIO", "NWC"),
        feature_group_count=CONV_WIDTH,
    ) + dw_b
    return jax.nn.silu(layernorm(y, conv_ln_g, conv_ln_b))


def trunk(x, ffn1_norm, ffn1_w_gu, ffn1_w_down, mix_norm, w_in, pool_w, pool_b, pool_scale,
          dw_w, dw_b, conv_ln_g, conv_ln_b, w_out, ffn2_norm, ffn2_w_gu, ffn2_w_down, final_norm):
    for l in range(DEPTH):
        x = x + 0.5 * swiglu_ffn(rmsnorm(x, ffn1_norm[l]), ffn1_w_gu[l], ffn1_w_down[l])
        u = rmsnorm(x, mix_norm[l]) @ w_in[l]
        u_pool = u[..., :POOL_WIDTH]
        u_val = u[..., POOL_WIDTH:POOL_WIDTH + CONV_WIDTH]
        u_gate = u[..., POOL_WIDTH + CONV_WIDTH:]
        a = pool_mixer(u_pool, pool_w[l], pool_b[l], pool_scale[l])
        b = conv_mixer(u_val, u_gate, dw_w[l], dw_b[l], conv_ln_g[l], conv_ln_b[l])
        x = x + jnp.concatenate([a, b], axis=-1) @ w_out[l]
        x = x + 0.5 * swiglu_ffn(rmsnorm(x, ffn2_norm[l]), ffn2_w_gu[l], ffn2_w_down[l])
    return rmsnorm(x, final_norm)


def setup_inputs(seed: int = 0) -> dict:
    key = jax.random.key(seed)
    ks = jax.random.split(key, 20)
    f32 = jnp.float32

    def nrm(k, shape, scale):
        return jax.random.normal(k, shape, f32) * scale

    return {
        "x_prompt": nrm(ks[0], (BATCH, SEQ, D_MODEL), 1.0),
        "x_sample": nrm(ks[1], (DEC_BATCH, DEC_SEQ, D_MODEL), 1.0),
        "ffn1_norm": 1.0 + nrm(ks[2], (DEPTH, D_MODEL), 0.02),
        "ffn1_w_gu": nrm(ks[3], (DEPTH, D_MODEL, 2 * D_FF), D_MODEL ** -0.5),
        "ffn1_w_down": nrm(ks[4], (DEPTH, D_FF, D_MODEL), D_FF ** -0.5),
        "mix_norm": 1.0 + nrm(ks[5], (DEPTH, D_MODEL), 0.02),
        "w_in": nrm(ks[6], (DEPTH, D_MODEL, IN_COLS), D_MODEL ** -0.5),
        "pool_w": nrm(ks[7], (DEPTH, N_POOL_GROUPS, POOL_GROUP_DIM, POOL_GROUP_DIM), POOL_GROUP_DIM ** -0.5),
        "pool_b": nrm(ks[8], (DEPTH, N_POOL_GROUPS, POOL_GROUP_DIM), 0.02),
        "pool_scale": 1.0 + nrm(ks[9], (DEPTH, POOL_WIDTH), 0.1),
        "dw_w": nrm(ks[10], (DEPTH, CONV_KERNEL, CONV_WIDTH), CONV_KERNEL ** -0.5),
        "dw_b": nrm(ks[11], (DEPTH, CONV_WIDTH), 0.02),
        "conv_ln_g": 1.0 + nrm(ks[12], (DEPTH, CONV_WIDTH), 0.02),
        "conv_ln_b": nrm(ks[13], (DEPTH, CONV_WIDTH), 0.02),
        "w_out": nrm(ks[14], (DEPTH, MIX_WIDTH, D_MODEL), MIX_WIDTH ** -0.5),
        "ffn2_norm": 1.0 + nrm(ks[15], (DEPTH, D_MODEL), 0.02),
        "ffn2_w_gu": nrm(ks[16], (DEPTH, D_MODEL, 2 * D_FF), D_MODEL ** -0.5),
        "ffn2_w_down": nrm(ks[17], (DEPTH, D_FF, D_MODEL), D_FF ** -0.5),
        "final_norm": 1.0 + nrm(ks[18], (D_MODEL,), 0.02),
    }


def reference(x_prompt, x_sample, ffn1_norm, ffn1_w_gu, ffn1_w_down, mix_norm, w_in, pool_w, pool_b,
              pool_scale, dw_w, dw_b, conv_ln_g, conv_ln_b, w_out, ffn2_norm, ffn2_w_gu, ffn2_w_down,
              final_norm):
    y_prompt = trunk(x_prompt, ffn1_norm, ffn1_w_gu, ffn1_w_down, mix_norm, w_in, pool_w, pool_b,
                     pool_scale, dw_w, dw_b, conv_ln_g, conv_ln_b, w_out, ffn2_norm, ffn2_w_gu,
                     ffn2_w_down, final_norm)
    y_sample = trunk(x_sample, ffn1_norm, ffn1_w_gu, ffn1_w_down, mix_norm, w_in, pool_w, pool_b,
                     pool_scale, dw_w, dw_b, conv_ln_g, conv_ln_b, w_out, ffn2_norm, ffn2_w_gu,
                     ffn2_w_down, final_norm)
    return (y_prompt, y_sample)
```

```python
import functools

import jax
import jax.numpy as jnp
from jax import lax
from jax.experimental import pallas as pl
from jax.experimental.pallas import tpu as pltpu

EPS = 1e-6
POOL_WINDOWS = (2, 4, 8, 16)
CONV_KERNEL = 31
CONV_HALF = CONV_KERNEL // 2

SUBLANES_F32 = 8
SUBLANES_BF16 = 16
LANES = 128
MXU_COLS_V7X = 256
VMEM_BYTES_V7X = 64 * 1024 * 1024

HALO = SUBLANES_BF16
assert HALO >= CONV_HALF and HALO >= max(POOL_WINDOWS) // 2

FFN_TOKENS = 512
FFN_CHUNK = 256
MIX_TOKENS = 512


def _vmem_limit(estimate_bytes):
    return int(min(estimate_bytes * 3 // 2 + (8 << 20), VMEM_BYTES_V7X - (6 << 20)))


def _rms(x, gain):
    ms = jnp.mean(x * x, axis=-1, keepdims=True)
    return x * lax.rsqrt(ms + EPS) * gain


def _resident():
    return pl.BlockSpec(memory_space=pltpu.VMEM)


def _ffn_kernel(x_ref, gain_ref, wgu_ref, wd_ref, fgain_ref, o_ref, h_ref, *, n_chunks, chunk, final_norm):
    h_ref[...] = _rms(x_ref[...], gain_ref[...]).astype(jnp.bfloat16)
    acc = None
    for c in range(n_chunks):
        gu = jnp.dot(h_ref[...], wgu_ref[c], preferred_element_type=jnp.float32)
        g = gu[:, :chunk]
        u = gu[:, chunk:]
        a = (g * jax.nn.sigmoid(g) * u).astype(jnp.bfloat16)
        d = jnp.dot(a, wd_ref[c], preferred_element_type=jnp.float32)
        acc = d if acc is None else acc + d
    y = x_ref[...] + 0.5 * acc
    if final_norm:
        y = _rms(y, fgain_ref[...])
    o_ref[...] = y


def _ffn(x2d, gain, wgu, wd, fgain, *, final_norm):
    n, d = x2d.shape
    n_chunks, _, two_chunk = wgu.shape
    chunk = two_chunk // 2
    t = FFN_TOKENS
    assert n % t == 0
    est = (4 * t * d * 4
           + wgu.size * 2 + wd.size * 2
           + t * d * 2 + t * d * 4
           + 3 * t * two_chunk * 4)
    kern = functools.partial(_ffn_kernel, n_chunks=n_chunks, chunk=chunk, final_norm=final_norm)
    return pl.pallas_call(
        kern,
        grid=(n // t,),
        in_specs=[
            pl.BlockSpec((t, d), lambda i: (i, 0)),
            _resident(), _resident(), _resident(), _resident(),
        ],
        out_specs=pl.BlockSpec((t, d), lambda i: (i, 0)),
        out_shape=jax.ShapeDtypeStruct((n, d), jnp.float32),
        scratch_shapes=[pltpu.VMEM((t, d), jnp.bfloat16)],
        compiler_params=pltpu.CompilerParams(
            dimension_semantics=("arbitrary",), vmem_limit_bytes=_vmem_limit(est)),
        name="ffn_final" if final_norm else "ffn",
    )(x2d, gain, wgu, wd, fgain)


def _mixer_kernel(xl_ref, xm_ref, xr_ref, gain_ref, win_ref, pw_ref, pb_ref, ps_ref, dww_ref, dwb_ref,
                  lng_ref, lnb_ref, wout_ref, o_ref, h_ref, p_ref, v_ref, *, tokens, seq_len, pool_width):
    t = tokens
    ext = t + 2 * HALO
    i = pl.program_id(1)
    gain = gain_ref[...]
    h_ref[0:HALO, :] = _rms(xl_ref[0], gain).astype(jnp.bfloat16)
    h_ref[HALO:HALO + t, :] = _rms(xm_ref[0], gain).astype(jnp.bfloat16)
    h_ref[HALO + t:ext, :] = _rms(xr_ref[0], gain).astype(jnp.bfloat16)
    u = jnp.dot(h_ref[...], win_ref[...], preferred_element_type=jnp.float32)

    conv_width = (u.shape[1] - pool_width) // 2
    pos_ext = i * t - HALO + lax.broadcasted_iota(jnp.int32, (ext, 1), 0)
    inside = (pos_ext >= 0) & (pos_ext < seq_len)
    p_ref[...] = jnp.where(inside, u[:, :pool_width], 0.0)
    u_val = u[:, pool_width:pool_width + conv_width]
    u_gate = u[:, pool_width + conv_width:]
    v_ref[...] = jnp.where(inside, u_val * jax.nn.sigmoid(u_gate), 0.0)

    pos = i * t + lax.broadcasted_iota(jnp.int32, (t, 1), 0)
    group = pool_width // len(POOL_WINDOWS)
    pooled = []
    for gi, w in enumerate(POOL_WINDOWS):
        half = w // 2
        cols = slice(gi * group, (gi + 1) * group)
        s = p_ref[pl.ds(HALO - half, t), cols]
        for j in range(-half + 1, half):
            s = s + p_ref[pl.ds(HALO + j, t), cols]
        cnt = (jnp.clip(pos + half, 0, seq_len) - jnp.clip(pos - half, 0, seq_len)).astype(jnp.float32)
        pooled.append(s / cnt - p_ref[pl.ds(HALO, t), cols])
    pooled = jnp.concatenate(pooled, axis=1).astype(jnp.bfloat16)
    a = (jnp.dot(pooled, pw_ref[...], preferred_element_type=jnp.float32) + pb_ref[...]) * ps_ref[...]

    y = v_ref[pl.ds(HALO - CONV_HALF, t), :] * dww_ref[0:1, :] + dwb_ref[...]
    for k in range(1, CONV_KERNEL):
        y = y + v_ref[pl.ds(HALO - CONV_HALF + k, t), :] * dww_ref[k:k + 1, :]
    mu = jnp.mean(y, axis=-1, keepdims=True)
    yc = y - mu
    var = jnp.mean(yc * yc, axis=-1, keepdims=True)
    ln = yc * lax.rsqrt(var + EPS) * lng_ref[...] + lnb_ref[...]
    b = ln * jax.nn.sigmoid(ln)

    cat = jnp.concatenate([a, b], axis=1).astype(jnp.bfloat16)
    o_ref[0] = xm_ref[0] + jnp.dot(cat, wout_ref[...], preferred_element_type=jnp.float32)


def _mixer(x3d, gain, win, pw_bd, pb, ps, dww, dwb, lng, lnb, wout):
    bsz, seq_len, d = x3d.shape
    t = MIX_TOKENS
    assert seq_len % t == 0 and t % HALO == 0
    ext = t + 2 * HALO
    pool_width = pw_bd.shape[0]
    conv_width = dww.shape[1]
    halo_blocks = t // HALO
    last_halo_block = seq_len // HALO - 1
    est = (4 * t * d * 4 + 8 * HALO * d * 4
           + (win.size + pw_bd.size + wout.size) * 2
           + ext * d * 2 + ext * (pool_width + conv_width) * 4
           + 2 * ext * win.shape[1] * 4 + 4 * t * d * 4)
    kern = functools.partial(_mixer_kernel, tokens=t, seq_len=seq_len, pool_width=pool_width)
    return pl.pallas_call(
        kern,
        grid=(bsz, seq_len // t),
        in_specs=[
            pl.BlockSpec((1, HALO, d), lambda b, i: (b, jnp.maximum(i * halo_blocks - 1, 0), 0)),
            pl.BlockSpec((1, t, d), lambda b, i: (b, i, 0)),
            pl.BlockSpec((1, HALO, d), lambda b, i: (b, jnp.minimum((i + 1) * halo_blocks, last_halo_block), 0)),
        ] + [_resident()] * 10,
        out_specs=pl.BlockSpec((1, t, d), lambda b, i: (b, i, 0)),
        out_shape=jax.ShapeDtypeStruct((bsz, seq_len, d), jnp.float32),
        scratch_shapes=[
            pltpu.VMEM((ext, d), jnp.bfloat16),
            pltpu.VMEM((ext, pool_width), jnp.float32),
            pltpu.VMEM((ext, conv_width), jnp.float32),
        ],
        compiler_params=pltpu.CompilerParams(
            dimension_semantics=("arbitrary", "arbitrary"), vmem_limit_bytes=_vmem_limit(est)),
        name="mixer",
    )(x3d, x3d, x3d, gain, win, pw_bd, pb, ps, dww, dwb, lng, lnb, wout)


def _chunk_ffn_weights(w_gu, w_down):
    d, two_f = w_gu.shape
    f = two_f // 2
    assert f % FFN_CHUNK == 0 and FFN_CHUNK % MXU_COLS_V7X == 0
    n_chunks = f // FFN_CHUNK
    wg = w_gu[:, :f].reshape(d, n_chunks, FFN_CHUNK)
    wu = w_gu[:, f:].reshape(d, n_chunks, FFN_CHUNK)
    wgu = jnp.concatenate([wg, wu], axis=2).transpose(1, 0, 2).astype(jnp.bfloat16)
    wd = w_down.reshape(n_chunks, FFN_CHUNK, d).astype(jnp.bfloat16)
    return wgu, wd


def _block_diag(pool_w):
    g, c, _ = pool_w.shape
    eye = jnp.eye(g, dtype=pool_w.dtype)
    return jnp.einsum("gcd,gh->gchd", pool_w, eye).reshape(g * c, g * c)


def kernel(x_prompt, x_sample, ffn1_norm, ffn1_w_gu, ffn1_w_down, mix_norm, w_in, pool_w, pool_b, pool_scale,
           dw_w, dw_b, conv_ln_g, conv_ln_b, w_out, ffn2_norm, ffn2_w_gu, ffn2_w_down, final_norm):
    depth = ffn1_norm.shape[0]
    row = lambda v: v.reshape(1, -1)
    layers = []
    for l in range(depth):
        layers.append(dict(
            ffn1=(row(ffn1_norm[l]),) + _chunk_ffn_weights(ffn1_w_gu[l], ffn1_w_down[l]),
            ffn2=(row(ffn2_norm[l]),) + _chunk_ffn_weights(ffn2_w_gu[l], ffn2_w_down[l]),
            mix=(row(mix_norm[l]), w_in[l].astype(jnp.bfloat16), _block_diag(pool_w[l]).astype(jnp.bfloat16),
                 row(pool_b[l]), row(pool_scale[l]), dw_w[l], row(dw_b[l]), row(conv_ln_g[l]),
                 row(conv_ln_b[l]), w_out[l].astype(jnp.bfloat16)),
        ))
    fgain = row(final_norm)

    def trunk(x):
        bsz, seq_len, d = x.shape
        for l, lw in enumerate(layers):
            last = l == depth - 1
            x = _ffn(x.reshape(bsz * seq_len, d), *lw["ffn1"], fgain, final_norm=False)
            x = _mixer(x.reshape(bsz, seq_len, d), *lw["mix"])
            x = _ffn(x.reshape(bsz * seq_len, d), *lw["ffn2"], fgain, final_norm=last)
            x = x.reshape(bsz, seq_len, d)
        return x

    return trunk(x_prompt), trunk(x_sample)
```

```python
import functools

import jax
import jax.numpy as jnp
from jax import lax
from jax.experimental import pallas as pl
from jax.experimental.pallas import tpu as pltpu

EPS = 1e-6
POOL_WINDOWS = (2, 4, 8, 16)
CONV_KERNEL = 31
CONV_HALF = CONV_KERNEL // 2

SUBLANES_F32 = 8
SUBLANES_BF16 = 16
LANES = 128
MXU_COLS_V7X = 256
VMEM_BYTES_V7X = 64 * 1024 * 1024

HALO = SUBLANES_BF16
POOL_REACH = max(POOL_WINDOWS) // 2
assert HALO >= CONV_HALF + 1 and POOL_REACH == SUBLANES_F32

FFN_TOKENS = 512
FFN_CHUNK = 256
MIX_TOKENS = 512
ROW_CHUNK = 128


def _vmem_limit(estimate_bytes):
    return int(min(estimate_bytes * 3 // 2 + (8 << 20), VMEM_BYTES_V7X - (6 << 20)))


def _rms(x, gain):
    ms = jnp.mean(x * x, axis=-1, keepdims=True)
    return x * lax.rsqrt(ms + EPS) * gain


def _resident():
    return pl.BlockSpec(memory_space=pltpu.VMEM)


def _shift_rows_up(x, r):
    return x if r == 0 else pltpu.roll(x, x.shape[0] - r, axis=0)


def _shift_rows_down(x, r):
    return x if r == 0 else pltpu.roll(x, r, axis=0)


def _ffn_kernel(x_ref, gain_ref, wgu_ref, wd_ref, ngain_ref, *out_and_scratch, n_chunks, chunk, final_norm):
    if final_norm:
        o_ref, h_ref = out_and_scratch
    else:
        o_ref, hn_ref, h_ref = out_and_scratch
    h_ref[...] = _rms(x_ref[...], gain_ref[...]).astype(jnp.bfloat16)
    acc = None
    for c in range(n_chunks):
        gu = jnp.dot(h_ref[...], wgu_ref[c], preferred_element_type=jnp.float32)
        g = gu[:, :chunk]
        u = gu[:, chunk:]
        a = (g * jax.nn.sigmoid(g) * u).astype(jnp.bfloat16)
        d = jnp.dot(a, wd_ref[c], preferred_element_type=jnp.float32)
        acc = d if acc is None else acc + d
    y = x_ref[...] + 0.5 * acc
    yn = _rms(y, ngain_ref[...])
    if final_norm:
        o_ref[...] = yn
    else:
        o_ref[...] = y
        hn_ref[...] = yn.astype(jnp.bfloat16)


def _ffn(x2d, gain, wgu, wd, ngain, *, final_norm):
    n, d = x2d.shape
    n_chunks, _, two_chunk = wgu.shape
    chunk = two_chunk // 2
    t = FFN_TOKENS
    assert n % t == 0
    est = (4 * t * d * 4 + 2 * t * d * 2
           + wgu.size * 2 + wd.size * 2
           + t * d * 2 + t * d * 4
           + 3 * t * two_chunk * 4)
    tile = pl.BlockSpec((t, d), lambda i: (i, 0))
    out_shape = jax.ShapeDtypeStruct((n, d), jnp.float32)
    if final_norm:
        out_specs, out_shapes = tile, out_shape
    else:
        out_specs, out_shapes = (tile, tile), (out_shape, jax.ShapeDtypeStruct((n, d), jnp.bfloat16))
    kern = functools.partial(_ffn_kernel, n_chunks=n_chunks, chunk=chunk, final_norm=final_norm)
    return pl.pallas_call(
        kern,
        grid=(n // t,),
        in_specs=[tile, _resident(), _resident(), _resident(), _resident()],
        out_specs=out_specs,
        out_shape=out_shapes,
        scratch_shapes=[pltpu.VMEM((t, d), jnp.bfloat16)],
        compiler_params=pltpu.CompilerParams(
            dimension_semantics=("arbitrary",), vmem_limit_bytes=_vmem_limit(est)),
        name="ffn_final" if final_norm else "ffn",
    )(x2d, gain, wgu, wd, ngain)


def _pool_chunk(p_ref, base, cols, window, pos, seq_len):
    r = ROW_CHUNK
    x = p_ref[pl.ds(base + HALO - POOL_REACH, r + 2 * POOL_REACH), cols]
    s = x + _shift_rows_down(x, 1)
    w = 2
    while w < window:
        s = _shift_rows_down(s, w // 2) + _shift_rows_up(s, w // 2)
        w *= 2
    half = window // 2
    cnt = window - jnp.maximum(half - pos, 0) - jnp.maximum(pos + half - seq_len, 0)
    mid = slice(POOL_REACH, POOL_REACH + r)
    return s[mid] / cnt.astype(jnp.float32) - x[mid]


def _conv_chunk(v_ref, dww_ref, dwb_ref, base, cols):
    r = ROW_CHUNK
    y = None
    for j in range(SUBLANES_F32):
        z = None
        for q in range((CONV_KERNEL + 1) // SUBLANES_F32):
            s = SUBLANES_F32 * q + j
            if s == 0:
                continue
            term = v_ref[pl.ds(base + SUBLANES_F32 * q, r + SUBLANES_F32), cols] * dww_ref[s - 1:s, cols]
            z = term if z is None else z + term
        z = _shift_rows_up(z, j)[:r]
        y = z if y is None else y + z
    return y + dwb_ref[:, cols]


def _mixer_kernel(hl_ref, hm_ref, hr_ref, x_ref, win_ref, pw_ref, pb_ref, ps_ref, dww_ref, dwb_ref,
                  lng_ref, lnb_ref, wout_ref, o_ref, h_ref, p_ref, v_ref, pooled_ref, y_ref,
                  *, tokens, seq_len, pool_width):
    t = tokens
    ext = t + 2 * HALO
    i = pl.program_id(1)
    h_ref[0:HALO, :] = hl_ref[0]
    h_ref[HALO:HALO + t, :] = hm_ref[0]
    h_ref[HALO + t:ext, :] = hr_ref[0]

    conv_width = (win_ref.shape[1] - pool_width) // 2
    pos_ext = i * t - HALO + lax.broadcasted_iota(jnp.int32, (ext, 1), 0)
    inside = (pos_ext >= 0) & (pos_ext < seq_len)
    h = h_ref[...]
    u_pool = jnp.dot(h, win_ref[:, :pool_width], preferred_element_type=jnp.float32)
    p_ref[...] = jnp.where(inside, u_pool, 0.0)
    u_val = jnp.dot(h, win_ref[:, pool_width:pool_width + conv_width], preferred_element_type=jnp.float32)
    u_gate = jnp.dot(h, win_ref[:, pool_width + conv_width:], preferred_element_type=jnp.float32)
    v_ref[...] = jnp.where(inside, u_val * jax.nn.sigmoid(u_gate), 0.0)

    group = pool_width // len(POOL_WINDOWS)
    row = lax.broadcasted_iota(jnp.int32, (ROW_CHUNK, LANES), 0)
    for base in range(0, t, ROW_CHUNK):
        pos = row + (i * t + base)
        for gi, window in enumerate(POOL_WINDOWS):
            for c0 in range(gi * group, (gi + 1) * group, LANES):
                cols = slice(c0, c0 + LANES)
                pooled = _pool_chunk(p_ref, base, cols, window, pos, seq_len)
                pooled_ref[base:base + ROW_CHUNK, cols] = pooled.astype(jnp.bfloat16)
        for c0 in range(0, conv_width, LANES):
            cols = slice(c0, c0 + LANES)
            y_ref[base:base + ROW_CHUNK, cols] = _conv_chunk(v_ref, dww_ref, dwb_ref, base, cols)

    a = (jnp.dot(pooled_ref[...], pw_ref[...], preferred_element_type=jnp.float32) + pb_ref[...]) * ps_ref[...]
    y = y_ref[...]
    mu = jnp.mean(y, axis=-1, keepdims=True)
    yc = y - mu
    var = jnp.mean(yc * yc, axis=-1, keepdims=True)
    ln = yc * lax.rsqrt(var + EPS) * lng_ref[...] + lnb_ref[...]
    b = ln * jax.nn.sigmoid(ln)

    cat = jnp.concatenate([a, b], axis=1).astype(jnp.bfloat16)
    o_ref[0] = x_ref[0] + jnp.dot(cat, wout_ref[...], preferred_element_type=jnp.float32)


def _mixer(x3d, h3d, win, pw_bd, pb, ps, dww, dwb, lng, lnb, wout):
    bsz, seq_len, d = x3d.shape
    t = MIX_TOKENS
    assert seq_len % t == 0 and t % HALO == 0 and t % ROW_CHUNK == 0
    ext = t + 2 * HALO
    pool_width = pw_bd.shape[0]
    conv_width = dww.shape[1]
    assert pool_width % (len(POOL_WINDOWS) * LANES) == 0 and conv_width % LANES == 0
    halo_blocks = t // HALO
    last_halo_block = seq_len // HALO - 1
    est = (4 * t * d * 4 + 2 * (t + 2 * HALO) * d * 2
           + (win.size + pw_bd.size + wout.size) * 2
           + ext * d * 2 + ext * (pool_width + conv_width) * 4 + t * pool_width * 2 + t * conv_width * 4
           + 2 * ext * win.shape[1] * 4 + 4 * t * d * 4)
    kern = functools.partial(_mixer_kernel, tokens=t, seq_len=seq_len, pool_width=pool_width)
    tile = lambda b, i: (b, i, 0)
    return pl.pallas_call(
        kern,
        grid=(bsz, seq_len // t),
        in_specs=[
            pl.BlockSpec((1, HALO, d), lambda b, i: (b, jnp.maximum(i * halo_blocks - 1, 0), 0)),
            pl.BlockSpec((1, t, d), tile),
            pl.BlockSpec((1, HALO, d), lambda b, i: (b, jnp.minimum((i + 1) * halo_blocks, last_halo_block), 0)),
            pl.BlockSpec((1, t, d), tile),
        ] + [_resident()] * 9,
        out_specs=pl.BlockSpec((1, t, d), tile),
        out_shape=jax.ShapeDtypeStruct((bsz, seq_len, d), jnp.float32),
        scratch_shapes=[
            pltpu.VMEM((ext, d), jnp.bfloat16),
            pltpu.VMEM((ext, pool_width), jnp.float32),
            pltpu.VMEM((ext, conv_width), jnp.float32),
            pltpu.VMEM((t, pool_width), jnp.bfloat16),
            pltpu.VMEM((t, conv_width), jnp.float32),
        ],
        compiler_params=pltpu.CompilerParams(
            dimension_semantics=("arbitrary", "arbitrary"), vmem_limit_bytes=_vmem_limit(est)),
        name="mixer",
    )(h3d, h3d, h3d, x3d, win, pw_bd, pb, ps, dww, dwb, lng, lnb, wout)


def _chunk_ffn_weights(w_gu, w_down):
    d, two_f = w_gu.shape
    f = two_f // 2
    assert f % FFN_CHUNK == 0 and FFN_CHUNK % MXU_COLS_V7X == 0
    n_chunks = f // FFN_CHUNK
    wg = w_gu[:, :f].reshape(d, n_chunks, FFN_CHUNK)
    wu = w_gu[:, f:].reshape(d, n_chunks, FFN_CHUNK)
    wgu = jnp.concatenate([wg, wu], axis=2).transpose(1, 0, 2).astype(jnp.bfloat16)
    wd = w_down.reshape(n_chunks, FFN_CHUNK, d).astype(jnp.bfloat16)
    return wgu, wd


def _block_diag(pool_w):
    g, c, _ = pool_w.shape
    eye = jnp.eye(g, dtype=pool_w.dtype)
    return jnp.einsum("gcd,gh->gchd", pool_w, eye).reshape(g * c, g * c)


def kernel(x_prompt, x_sample, ffn1_norm, ffn1_w_gu, ffn1_w_down, mix_norm, w_in, pool_w, pool_b, pool_scale,
           dw_w, dw_b, conv_ln_g, conv_ln_b, w_out, ffn2_norm, ffn2_w_gu, ffn2_w_down, final_norm):
    depth = ffn1_norm.shape[0]
    row = lambda v: v.reshape(1, -1)
    layers = []
    for l in range(depth):
        after = row(ffn1_norm[l + 1]) if l + 1 < depth else row(final_norm)
        layers.append(dict(
            ffn1=(row(ffn1_norm[l]),) + _chunk_ffn_weights(ffn1_w_gu[l], ffn1_w_down[l]) + (row(mix_norm[l]),),
            ffn2=(row(ffn2_norm[l]),) + _chunk_ffn_weights(ffn2_w_gu[l], ffn2_w_down[l]) + (after,),
            mix=(w_in[l].astype(jnp.bfloat16), _block_diag(pool_w[l]).astype(jnp.bfloat16),
                 row(pool_b[l]), row(pool_scale[l]), dw_w[l], row(dw_b[l]), row(conv_ln_g[l]),
                 row(conv_ln_b[l]), w_out[l].astype(jnp.bfloat16)),
        ))
    assert depth == 1, "layers after the first would need ffn1 to take its normalised input from the previous call"

    def trunk(x):
        bsz, seq_len, d = x.shape
        lw = layers[0]
        x, h = _ffn(x.reshape(bsz * seq_len, d), *lw["ffn1"], final_norm=False)
        x = _mixer(x.reshape(bsz, seq_len, d), h.reshape(bsz, seq_len, d), *lw["mix"])
        x = _ffn(x.reshape(bsz * seq_len, d), *lw["ffn2"], final_norm=True)
        return x.reshape(bsz, seq_len, d)

    return trunk(x_prompt), trunk(x_sample)
```

```python
import functools

import jax
import jax.numpy as jnp
from jax import lax
from jax.experimental import pallas as pl
from jax.experimental.pallas import tpu as pltpu

EPS = 1e-6
POOL_WINDOWS = (2, 4, 8, 16)
CONV_KERNEL = 31
CONV_HALF = CONV_KERNEL // 2

SUBLANES_F32 = 8
SUBLANES_BF16 = 16
LANES = 128
MXU_COLS_V7X = 256
VMEM_BYTES_V7X = 64 * 1024 * 1024

HALO = SUBLANES_BF16
POOL_REACH = max(POOL_WINDOWS) // 2
assert HALO >= CONV_HALF + 1 and POOL_REACH == SUBLANES_F32

TOKENS = 512
FFN_CHUNK = 256
LATE_HEAD_TASKS = 4
LATE_HEAD_START = 6
ROW_CHUNK = 64


def _vmem_limit(estimate_bytes):
    return int(min(estimate_bytes * 5 // 4 + (8 << 20), VMEM_BYTES_V7X - (6 << 20)))


def _rms(x, gain):
    ms = jnp.mean(x * x, axis=-1, keepdims=True)
    return x * lax.rsqrt(ms + EPS) * gain


def _resident():
    return pl.BlockSpec(memory_space=pltpu.VMEM)


def _shift_rows_up(x, r):
    return x if r == 0 else pltpu.roll(x, x.shape[0] - r, axis=0)


def _shift_rows_down(x, r):
    return x if r == 0 else pltpu.roll(x, r, axis=0)


def _zero_after(x):
    bits = lax.bitcast_convert_type(x, jnp.uint32)
    zero = lax.shift_right_logical(lax.shift_right_logical(bits, jnp.uint32(16)), jnp.uint32(16))
    return lax.bitcast_convert_type(zero, jnp.float32)


def _add_tile(x, tile):
    return x + jnp.tile(tile, (x.shape[0] // tile.shape[0], x.shape[1] // tile.shape[1]))


def _swiglu(h_ref, wgu_ref, wd_ref, late_tasks=(), late_start=0):
    n_chunks, _, two_chunk = wgu_ref.shape
    chunk = two_chunk // 2
    assert late_start + len(late_tasks) <= n_chunks
    acc = None
    for c in range(n_chunks):
        gu = jnp.dot(h_ref[...], wgu_ref[c], preferred_element_type=jnp.float32)
        g = gu[:, :chunk]
        u = gu[:, chunk:]
        if late_start <= c < late_start + len(late_tasks):
            late_tasks[c - late_start](_zero_after(gu[:SUBLANES_F32, :LANES]))
        a = (g * jax.nn.sigmoid(g) * u).astype(jnp.bfloat16)
        d = jnp.dot(a, wd_ref[c], preferred_element_type=jnp.float32)
        acc = d if acc is None else acc + d
    return acc


def _ffn_kernel(x_ref, gain_ref, wgu_ref, wd_ref, ngain_ref, o_ref, hn_ref, h_ref):
    h_ref[...] = _rms(x_ref[...], gain_ref[...]).astype(jnp.bfloat16)
    y = x_ref[...] + 0.5 * _swiglu(h_ref, wgu_ref, wd_ref)
    o_ref[...] = y
    hn_ref[...] = _rms(y, ngain_ref[...]).astype(jnp.bfloat16)


def _ffn(x2d, gain, wgu, wd, ngain):
    n, d = x2d.shape
    t = TOKENS
    assert n % t == 0
    est = (4 * t * d * 4 + 2 * t * d * 2
           + wgu.size * 2 + wd.size * 2
           + t * d * 2 + t * d * 4
           + 3 * t * wgu.shape[2] * 4)
    tile = pl.BlockSpec((t, d), lambda i: (i, 0))
    return pl.pallas_call(
        _ffn_kernel,
        grid=(n // t,),
        in_specs=[tile, _resident(), _resident(), _resident(), _resident()],
        out_specs=(tile, tile),
        out_shape=(jax.ShapeDtypeStruct((n, d), jnp.float32), jax.ShapeDtypeStruct((n, d), jnp.bfloat16)),
        scratch_shapes=[pltpu.VMEM((t, d), jnp.bfloat16)],
        compiler_params=pltpu.CompilerParams(
            dimension_semantics=("arbitrary",), vmem_limit_bytes=_vmem_limit(est)),
        name="ffn",
    )(x2d, gain, wgu, wd, ngain)


def _pool_chunk(p_ref, base, cols, window, pos, seq_len, zero):
    r = ROW_CHUNK
    x = _add_tile(p_ref[pl.ds(base + HALO - POOL_REACH, r + 2 * POOL_REACH), cols], zero)
    s = x + _shift_rows_down(x, 1)
    w = 2
    while w < window:
        s = _shift_rows_down(s, w // 2) + _shift_rows_up(s, w // 2)
        w *= 2
    half = window // 2
    cnt = window - jnp.maximum(half - pos, 0) - jnp.maximum(pos + half - seq_len, 0)
    mid = slice(POOL_REACH, POOL_REACH + r)
    return s[mid] / cnt.astype(jnp.float32) - x[mid]


def _conv_chunk(v_ref, dww_ref, dwb_ref, base, cols, zero):
    r = ROW_CHUNK
    v = _add_tile(v_ref[pl.ds(base, r + 2 * HALO), cols], zero)
    y = None
    for j in range(SUBLANES_F32):
        rows = r if j == 0 else r + SUBLANES_F32
        z = None
        for q in range((CONV_KERNEL + 1) // SUBLANES_F32):
            s = SUBLANES_F32 * q + j
            if s == 0:
                continue
            term = v[SUBLANES_F32 * q:SUBLANES_F32 * q + rows] * dww_ref[s - 1:s, cols]
            z = term if z is None else z + term
        z = _shift_rows_up(z, j)[:r]
        y = z if y is None else y + z
    return y + dwb_ref[:, cols]


def _mixer_project(i, hl_ref, hm_ref, hr_ref, win_ref, h_ref, p_ref, v_ref, *, seq_len):
    t = hm_ref.shape[1]
    ext = t + 2 * HALO
    pool_width = p_ref.shape[1]
    conv_width = v_ref.shape[1]
    h_ref[0:HALO, :] = hl_ref[0]
    h_ref[HALO:HALO + t, :] = hm_ref[0]
    h_ref[HALO + t:ext, :] = hr_ref[0]

    pos_ext = i * t - HALO + lax.broadcasted_iota(jnp.int32, (ext, 1), 0)
    inside = (pos_ext >= 0) & (pos_ext < seq_len)
    h = h_ref[...]
    u_pool = jnp.dot(h, win_ref[:, :pool_width], preferred_element_type=jnp.float32)
    p_ref[...] = jnp.where(inside, u_pool, 0.0)
    u_val = jnp.dot(h, win_ref[:, pool_width:pool_width + conv_width], preferred_element_type=jnp.float32)
    u_gate = jnp.dot(h, win_ref[:, pool_width + conv_width:], preferred_element_type=jnp.float32)
    v = jnp.where(inside, u_val * jax.nn.sigmoid(u_gate), 0.0)
    v_ref[...] = v
    return _zero_after(v[:SUBLANES_F32, :LANES]) + _zero_after(v[-SUBLANES_F32:, -LANES:])


def _mixer_window_tasks(i, p_ref, v_ref, dww_ref, dwb_ref, pooled_ref, y_ref, *, seq_len):
    t, pool_width = pooled_ref.shape
    conv_width = y_ref.shape[1]
    group = pool_width // len(POOL_WINDOWS)
    row = lax.broadcasted_iota(jnp.int32, (ROW_CHUNK, LANES), 0)

    def pool_task(base, cols, window, zero):
        pos = row + (i * t + base)
        pooled = _pool_chunk(p_ref, base, cols, window, pos, seq_len, zero)
        pooled_ref[base:base + ROW_CHUNK, cols] = pooled.astype(jnp.bfloat16)

    def conv_task(base, cols, zero):
        y_ref[base:base + ROW_CHUNK, cols] = _conv_chunk(v_ref, dww_ref, dwb_ref, base, cols, zero)

    pool_tasks, conv_tasks = [], []
    for base in range(0, t, ROW_CHUNK):
        for gi, window in enumerate(POOL_WINDOWS):
            for c0 in range(gi * group, (gi + 1) * group, LANES):
                pool_tasks.append(functools.partial(pool_task, base, slice(c0, c0 + LANES), window))
        for c0 in range(0, conv_width, LANES):
            conv_tasks.append(functools.partial(conv_task, base, slice(c0, c0 + LANES)))
    return pool_tasks + conv_tasks


def _pool_head(pooled_ref, pw_ref, pb_ref, ps_ref, cat_ref):
    pool_width = pooled_ref.shape[1]
    a = (jnp.dot(pooled_ref[...], pw_ref[...], preferred_element_type=jnp.float32) + pb_ref[...]) * ps_ref[...]
    cat_ref[:, :pool_width] = a.astype(jnp.bfloat16)


def _conv_head_tasks(y_ref, lng_ref, lnb_ref, cat_ref, n_tasks):
    t, conv_width = y_ref.shape
    pool_width = cat_ref.shape[1] - conv_width
    rows = t // n_tasks

    def task(r0, zero):
        y = _add_tile(y_ref[r0:r0 + rows, :], zero)
        mu = jnp.mean(y, axis=-1, keepdims=True)
        yc = y - mu
        var = jnp.mean(yc * yc, axis=-1, keepdims=True)
        ln = yc * lax.rsqrt(var + EPS) * lng_ref[...] + lnb_ref[...]
        cat_ref[r0:r0 + rows, pool_width:] = (ln * jax.nn.sigmoid(ln)).astype(jnp.bfloat16)

    return [functools.partial(task, r0) for r0 in range(0, t, rows)]


def _mixer_ffn_kernel(hl_ref, hm_ref, hr_ref, x_ref, win_ref, pw_ref, pb_ref, ps_ref, dww_ref, dwb_ref,
                      lng_ref, lnb_ref, wout_ref, gain_ref, wgu_ref, wd_ref, fgain_ref, o_ref,
                      h_ref, p_ref, v_ref, pooled_ref, y_ref, yprev_ref, cat_ref, x2_ref, hf_ref,
                      *, n_tiles, tiles_per_seq, seq_len):
    g = pl.program_id(0)

    @pl.when(g == 0)
    def _():
        pooled_ref[...] = jnp.zeros_like(pooled_ref)
        y_ref[...] = jnp.zeros_like(y_ref)
        x2_ref[...] = jnp.zeros_like(x2_ref)

    x2 = x2_ref[...]
    hf_ref[...] = _rms(x2, gain_ref[...]).astype(jnp.bfloat16)
    o_ref[0] = x2

    _pool_head(pooled_ref, pw_ref, pb_ref, ps_ref, cat_ref)
    yprev_ref[...] = y_ref[...]
    conv_head = _conv_head_tasks(yprev_ref, lng_ref, lnb_ref, cat_ref, LATE_HEAD_TASKS)

    i = lax.rem(jnp.minimum(g, n_tiles - 1), tiles_per_seq)
    glu_done = _mixer_project(i, hl_ref, hm_ref, hr_ref, win_ref, h_ref, p_ref, v_ref, seq_len=seq_len)
    for task in _mixer_window_tasks(i, p_ref, v_ref, dww_ref, dwb_ref, pooled_ref, y_ref, seq_len=seq_len):
        task(glu_done)
    acc = _swiglu(hf_ref, wgu_ref, wd_ref, late_tasks=conv_head, late_start=LATE_HEAD_START)
    x2_ref[...] = x_ref[0] + jnp.dot(cat_ref[...], wout_ref[...], preferred_element_type=jnp.float32)
    o_ref[0] = _rms(o_ref[0] + 0.5 * acc, fgain_ref[...])


def _mixer_ffn(x3d, h3d, win, pw_bd, pb, ps, dww, dwb, lng, lnb, wout, gain, wgu, wd, fgain):
    bsz, seq_len, d = x3d.shape
    t = TOKENS
    assert seq_len % t == 0 and t % HALO == 0 and t % ROW_CHUNK == 0 and t % LATE_HEAD_TASKS == 0
    ext = t + 2 * HALO
    pool_width = pw_bd.shape[0]
    conv_width = dww.shape[1]
    assert pool_width % (len(POOL_WINDOWS) * LANES) == 0 and conv_width % LANES == 0
    tiles_per_seq = seq_len // t
    n_tiles = bsz * tiles_per_seq
    halo_blocks = t // HALO
    last_halo_block = seq_len // HALO - 1
    est = (4 * t * d * 4 + 2 * ext * d * 2
           + (win.size + pw_bd.size + wout.size + wgu.size + wd.size) * 2
           + ext * d * 2 + ext * (pool_width + conv_width) * 4 + t * pool_width * 2 + 2 * t * conv_width * 4
           + t * (pool_width + conv_width) * 2
           + t * d * 4 + t * d * 2
           + ext * win.shape[1] * 4 + 2 * t * d * 4 + 3 * t * wgu.shape[2] * 4)

    def stage_tile(g, lag):
        tile = jnp.clip(g - lag, 0, n_tiles - 1)
        return tile // tiles_per_seq, tile % tiles_per_seq

    def stage_block(lag):
        def index_map(g):
            b, i = stage_tile(g, lag)
            return b, i, 0
        return index_map

    def left_halo_block(g):
        b, i = stage_tile(g, 0)
        return b, jnp.maximum(i * halo_blocks - 1, 0), 0

    def right_halo_block(g):
        b, i = stage_tile(g, 0)
        return b, jnp.minimum((i + 1) * halo_blocks, last_halo_block), 0

    kern = functools.partial(_mixer_ffn_kernel, n_tiles=n_tiles, tiles_per_seq=tiles_per_seq, seq_len=seq_len)
    return pl.pallas_call(
        kern,
        grid=(n_tiles + 2,),
        in_specs=[
            pl.BlockSpec((1, HALO, d), left_halo_block),
            pl.BlockSpec((1, t, d), stage_block(0)),
            pl.BlockSpec((1, HALO, d), right_halo_block),
            pl.BlockSpec((1, t, d), stage_block(1)),
        ] + [_resident()] * 13,
        out_specs=pl.BlockSpec((1, t, d), stage_block(2)),
        out_shape=jax.ShapeDtypeStruct((bsz, seq_len, d), jnp.float32),
        scratch_shapes=[
            pltpu.VMEM((ext, d), jnp.bfloat16),
            pltpu.VMEM((ext, pool_width), jnp.float32),
            pltpu.VMEM((ext, conv_width), jnp.float32),
            pltpu.VMEM((t, pool_width), jnp.bfloat16),
            pltpu.VMEM((t, conv_width), jnp.float32),
            pltpu.VMEM((t, conv_width), jnp.float32),
            pltpu.VMEM((t, pool_width + conv_width), jnp.bfloat16),
            pltpu.VMEM((t, d), jnp.float32),
            pltpu.VMEM((t, d), jnp.bfloat16),
        ],
        compiler_params=pltpu.CompilerParams(
            dimension_semantics=("arbitrary",), vmem_limit_bytes=_vmem_limit(est)),
        name="mixer_ffn",
    )(h3d, h3d, h3d, x3d, win, pw_bd, pb, ps, dww, dwb, lng, lnb, wout, gain, wgu, wd, fgain)


def _chunk_ffn_weights(w_gu, w_down):
    d, two_f = w_gu.shape
    f = two_f // 2
    assert f % FFN_CHUNK == 0 and FFN_CHUNK % MXU_COLS_V7X == 0
    n_chunks = f // FFN_CHUNK
    wg = w_gu[:, :f].reshape(d, n_chunks, FFN_CHUNK)
    wu = w_gu[:, f:].reshape(d, n_chunks, FFN_CHUNK)
    wgu = jnp.concatenate([wg, wu], axis=2).transpose(1, 0, 2).astype(jnp.bfloat16)
    wd = w_down.reshape(n_chunks, FFN_CHUNK, d).astype(jnp.bfloat16)
    return wgu, wd


def _block_diag(pool_w):
    g, c, _ = pool_w.shape
    eye = jnp.eye(g, dtype=pool_w.dtype)
    return jnp.einsum("gcd,gh->gchd", pool_w, eye).reshape(g * c, g * c)


def kernel(x_prompt, x_sample, ffn1_norm, ffn1_w_gu, ffn1_w_down, mix_norm, w_in, pool_w, pool_b, pool_scale,
           dw_w, dw_b, conv_ln_g, conv_ln_b, w_out, ffn2_norm, ffn2_w_gu, ffn2_w_down, final_norm):
    assert ffn1_norm.shape[0] == 1, "single-layer block"
    row = lambda v: v.reshape(1, -1)
    ffn1 = (row(ffn1_norm[0]),) + _chunk_ffn_weights(ffn1_w_gu[0], ffn1_w_down[0]) + (row(mix_norm[0]),)
    mix = (w_in[0].astype(jnp.bfloat16), _block_diag(pool_w[0]).astype(jnp.bfloat16), row(pool_b[0]),
           row(pool_scale[0]), dw_w[0], row(dw_b[0]), row(conv_ln_g[0]), row(conv_ln_b[0]),
           w_out[0].astype(jnp.bfloat16))
    ffn2 = (row(ffn2_norm[0]),) + _chunk_ffn_weights(ffn2_w_gu[0], ffn2_w_down[0]) + (row(final_norm),)

    def trunk(x):
        bsz, seq_len, d = x.shape
        x1, h = _ffn(x.reshape(bsz * seq_len, d), *ffn1)
        return _mixer_ffn(x1.reshape(bsz, seq_len, d), h.reshape(bsz, seq_len, d), *mix, *ffn2)

    return trunk(x_prompt), trunk(x_sample)
```

```python
import functools

import jax
import jax.numpy as jnp
from jax import lax
from jax.experimental import pallas as pl
from jax.experimental.pallas import tpu as pltpu

EPS = 1e-6
POOL_WINDOWS = (2, 4, 8, 16)
CONV_KERNEL = 31
CONV_HALF = CONV_KERNEL // 2

SUBLANES_F32 = 8
SUBLANES_BF16 = 16
LANES = 128
MXU_COLS_V7X = 256
VMEM_BYTES_V7X = 64 * 1024 * 1024

HALO = SUBLANES_BF16
POOL_REACH = max(POOL_WINDOWS) // 2
assert HALO >= CONV_HALF + 1 and POOL_REACH == SUBLANES_F32

TOKENS = 512
FFN_CHUNK = 256
LATE_HEAD_TASKS = 4
LATE_HEAD_START = 6
ROW_CHUNK = 64


def _vmem_limit(estimate_bytes):
    return int(min(estimate_bytes * 5 // 4 + (8 << 20), VMEM_BYTES_V7X - (6 << 20)))


def _rms(x, gain):
    ms = jnp.mean(x * x, axis=-1, keepdims=True)
    return x * lax.rsqrt(ms + EPS) * gain


def _resident():
    return pl.BlockSpec(memory_space=pltpu.VMEM)


def _shift_rows_up(x, r):
    return x if r == 0 else pltpu.roll(x, x.shape[0] - r, axis=0)


def _shift_rows_down(x, r):
    return x if r == 0 else pltpu.roll(x, r, axis=0)


def _zero_after(x):
    bits = lax.bitcast_convert_type(x, jnp.uint32)
    zero = lax.shift_right_logical(lax.shift_right_logical(bits, jnp.uint32(16)), jnp.uint32(16))
    return lax.bitcast_convert_type(zero, jnp.float32)


def _add_tile(x, tile):
    return x + jnp.tile(tile, (x.shape[0] // tile.shape[0], x.shape[1] // tile.shape[1]))


def _swiglu(h_ref, a_ref, wgu_ref, wd_ref, late_tasks=(), late_start=0):
    f = wd_ref.shape[0]
    chunk = FFN_CHUNK
    n_chunks = f // chunk
    assert late_start + len(late_tasks) <= n_chunks
    for c in range(n_chunks):
        cols = slice(c * chunk, (c + 1) * chunk)
        g = jnp.dot(h_ref[...], wgu_ref[:, cols], preferred_element_type=jnp.float32)
        u = jnp.dot(h_ref[...], wgu_ref[:, f + c * chunk:f + (c + 1) * chunk], preferred_element_type=jnp.float32)
        if late_start <= c < late_start + len(late_tasks):
            late_tasks[c - late_start](_zero_after(g[:SUBLANES_F32, :LANES]))
        a_ref[:, cols] = (g * jax.nn.sigmoid(g) * u).astype(jnp.bfloat16)
    return jnp.dot(a_ref[...], wd_ref[...], preferred_element_type=jnp.float32)


def _ffn_kernel(x_ref, gain_ref, wgu_ref, wd_ref, ngain_ref, o_ref, hn_ref, h_ref, a_ref):
    h_ref[...] = _rms(x_ref[...], gain_ref[...]).astype(jnp.bfloat16)
    y = x_ref[...] + 0.5 * _swiglu(h_ref, a_ref, wgu_ref, wd_ref)
    o_ref[...] = y
    hn_ref[...] = _rms(y, ngain_ref[...]).astype(jnp.bfloat16)


def _ffn(x2d, gain, wgu, wd, ngain):
    n, d = x2d.shape
    t = TOKENS
    assert n % t == 0
    est = (4 * t * d * 4 + 2 * t * d * 2
           + wgu.size * 2 + wd.size * 2
           + t * d * 2 + t * wd.shape[0] * 2 + t * d * 4
           + 3 * t * 2 * FFN_CHUNK * 4)
    tile = pl.BlockSpec((t, d), lambda i: (i, 0))
    return pl.pallas_call(
        _ffn_kernel,
        grid=(n // t,),
        in_specs=[tile, _resident(), _resident(), _resident(), _resident()],
        out_specs=(tile, tile),
        out_shape=(jax.ShapeDtypeStruct((n, d), jnp.float32), jax.ShapeDtypeStruct((n, d), jnp.bfloat16)),
        scratch_shapes=[pltpu.VMEM((t, d), jnp.bfloat16), pltpu.VMEM((t, wd.shape[0]), jnp.bfloat16)],
        compiler_params=pltpu.CompilerParams(
            dimension_semantics=("arbitrary",), vmem_limit_bytes=_vmem_limit(est)),
        name="ffn",
    )(x2d, gain, wgu, wd, ngain)


def _pool_chunk(p_ref, base, cols, window, pos, seq_len, zero):
    r = ROW_CHUNK
    x = _add_tile(p_ref[pl.ds(base + HALO - POOL_REACH, r + 2 * POOL_REACH), cols], zero)
    s = x + _shift_rows_down(x, 1)
    w = 2
    while w < window:
        s = _shift_rows_down(s, w // 2) + _shift_rows_up(s, w // 2)
        w *= 2
    half = window // 2
    cnt = window - jnp.maximum(half - pos, 0) - jnp.maximum(pos + half - seq_len, 0)
    mid = slice(POOL_REACH, POOL_REACH + r)
    return s[mid] / cnt.astype(jnp.float32) - x[mid]


def _conv_chunk(v_ref, dww_ref, dwb_ref, base, cols, zero):
    r = ROW_CHUNK
    v = _add_tile(v_ref[pl.ds(base, r + 2 * HALO), cols], zero)
    y = None
    for j in range(SUBLANES_F32):
        rows = r if j == 0 else r + SUBLANES_F32
        z = None
        for q in range((CONV_KERNEL + 1) // SUBLANES_F32):
            s = SUBLANES_F32 * q + j
            if s == 0:
                continue
            term = v[SUBLANES_F32 * q:SUBLANES_F32 * q + rows] * dww_ref[s - 1:s, cols]
            z = term if z is None else z + term
        z = _shift_rows_up(z, j)[:r]
        y = z if y is None else y + z
    return y + dwb_ref[:, cols]


def _mixer_project(i, hl_ref, hm_ref, hr_ref, win_ref, h_ref, p_ref, v_ref, *, seq_len):
    t = hm_ref.shape[1]
    ext = t + 2 * HALO
    pool_width = p_ref.shape[1]
    conv_width = v_ref.shape[1]
    h_ref[0:HALO, :] = hl_ref[0]
    h_ref[HALO:HALO + t, :] = hm_ref[0]
    h_ref[HALO + t:ext, :] = hr_ref[0]

    pos_ext = i * t - HALO + lax.broadcasted_iota(jnp.int32, (ext, 1), 0)
    inside = (pos_ext >= 0) & (pos_ext < seq_len)
    h = h_ref[...]
    u_pool = jnp.dot(h, win_ref[:, :pool_width], preferred_element_type=jnp.float32)
    p_ref[...] = jnp.where(inside, u_pool, 0.0)
    u_val = jnp.dot(h, win_ref[:, pool_width:pool_width + conv_width], preferred_element_type=jnp.float32)
    u_gate = jnp.dot(h, win_ref[:, pool_width + conv_width:], preferred_element_type=jnp.float32)
    v = jnp.where(inside, u_val * jax.nn.sigmoid(u_gate), 0.0)
    v_ref[...] = v
    return _zero_after(v[:SUBLANES_F32, :LANES]) + _zero_after(v[-SUBLANES_F32:, -LANES:])


def _mixer_window_tasks(i, p_ref, v_ref, dww_ref, dwb_ref, pooled_ref, y_ref, *, seq_len):
    t, pool_width = pooled_ref.shape
    conv_width = y_ref.shape[1]
    group = pool_width // len(POOL_WINDOWS)
    row = lax.broadcasted_iota(jnp.int32, (ROW_CHUNK, LANES), 0)

    def pool_task(base, cols, window, zero):
        pos = row + (i * t + base)
        pooled = _pool_chunk(p_ref, base, cols, window, pos, seq_len, zero)
        pooled_ref[base:base + ROW_CHUNK, cols] = pooled.astype(jnp.bfloat16)

    def conv_task(base, cols, zero):
        y = _conv_chunk(v_ref, dww_ref, dwb_ref, base, cols, zero)
        y_ref[base:base + ROW_CHUNK, cols] = y
        return _zero_after(y[:SUBLANES_F32, :])

    pool_tasks, conv_tasks = [], []
    for base in range(0, t, ROW_CHUNK):
        for gi, window in enumerate(POOL_WINDOWS):
            for c0 in range(gi * group, (gi + 1) * group, LANES):
                pool_tasks.append(functools.partial(pool_task, base, slice(c0, c0 + LANES), window))
        for c0 in range(0, conv_width, LANES):
            conv_tasks.append(functools.partial(conv_task, base, slice(c0, c0 + LANES)))
    return pool_tasks, conv_tasks


def _pool_head(pooled_ref, pw_ref, pb_ref, ps_ref, cat_ref):
    pool_width = pooled_ref.shape[1]
    for c0 in range(0, pool_width, MXU_COLS_V7X):
        cols = slice(c0, c0 + MXU_COLS_V7X)
        a = jnp.dot(pooled_ref[:, cols], pw_ref[cols, cols], preferred_element_type=jnp.float32)
        cat_ref[:, cols] = ((a + pb_ref[:, cols]) * ps_ref[:, cols]).astype(jnp.bfloat16)


def _conv_head_tasks(y_ref, lng_ref, lnb_ref, cat_ref, n_tasks):
    t, conv_width = y_ref.shape
    pool_width = cat_ref.shape[1] - conv_width
    rows = t // n_tasks

    def task(r0, zero):
        y = _add_tile(y_ref[r0:r0 + rows, :], zero)
        mu = jnp.mean(y, axis=-1, keepdims=True)
        yc = y - mu
        var = jnp.mean(yc * yc, axis=-1, keepdims=True)
        ln = yc * lax.rsqrt(var + EPS) * lng_ref[...] + lnb_ref[...]
        cat_ref[r0:r0 + rows, pool_width:] = (ln * jax.nn.sigmoid(ln)).astype(jnp.bfloat16)

    return [functools.partial(task, r0) for r0 in range(0, t, rows)]


def _mixer_ffn_kernel(hl_ref, hm_ref, hr_ref, x_ref, win_ref, pw_ref, pb_ref, ps_ref, dww_ref, dwb_ref,
                      lng_ref, lnb_ref, wout_ref, gain_ref, wgu_ref, wd_ref, fgain_ref, o_ref,
                      h_ref, p_ref, v_ref, pooled_ref, y_ref, yprev_ref, cat_ref, x2_ref, hf_ref, a_ref,
                      *, n_tiles, tiles_per_seq, seq_len):
    g = pl.program_id(0)

    @pl.when(g == 0)
    def _():
        pooled_ref[...] = jnp.zeros_like(pooled_ref)
        y_ref[...] = jnp.zeros_like(y_ref)
        x2_ref[...] = jnp.zeros_like(x2_ref)

    x2 = x2_ref[...]
    hf = _rms(x2, gain_ref[...])
    hf_ref[...] = hf.astype(jnp.bfloat16)
    o_ref[0] = x2
    ffn_input_done = _zero_after(hf[:SUBLANES_F32, :LANES]) + _zero_after(hf[-SUBLANES_F32:, -LANES:])

    _pool_head(pooled_ref, pw_ref, pb_ref, ps_ref, cat_ref)
    yprev_ref[...] = y_ref[...]
    conv_head = _conv_head_tasks(yprev_ref, lng_ref, lnb_ref, cat_ref, LATE_HEAD_TASKS)

    i = lax.rem(jnp.minimum(g, n_tiles - 1), tiles_per_seq)
    glu_done = _mixer_project(i, hl_ref, hm_ref, hr_ref, win_ref, h_ref, p_ref, v_ref, seq_len=seq_len)
    pool_tasks, conv_tasks = _mixer_window_tasks(i, p_ref, v_ref, dww_ref, dwb_ref, pooled_ref, y_ref, seq_len=seq_len)
    assert len(pool_tasks) == len(conv_tasks)
    conv_done = [task(glu_done + ffn_input_done) for task in conv_tasks]
    for task, after in zip(pool_tasks, conv_done):
        task(after)
    acc = _swiglu(hf_ref, a_ref, wgu_ref, wd_ref, late_tasks=conv_head, late_start=LATE_HEAD_START)
    x2_ref[...] = x_ref[0] + jnp.dot(cat_ref[...], wout_ref[...], preferred_element_type=jnp.float32)
    o_ref[0] = _rms(o_ref[0] + 0.5 * acc, fgain_ref[...])


def _mixer_ffn(x3d, h3d, win, pw_bd, pb, ps, dww, dwb, lng, lnb, wout, gain, wgu, wd, fgain):
    bsz, seq_len, d = x3d.shape
    t = TOKENS
    assert seq_len % t == 0 and t % HALO == 0 and t % ROW_CHUNK == 0 and t % LATE_HEAD_TASKS == 0
    ext = t + 2 * HALO
    pool_width = pw_bd.shape[0]
    conv_width = dww.shape[1]
    assert pool_width % (len(POOL_WINDOWS) * LANES) == 0 and conv_width % LANES == 0
    assert pool_width % MXU_COLS_V7X == 0 and MXU_COLS_V7X % (pool_width // len(POOL_WINDOWS)) == 0
    tiles_per_seq = seq_len // t
    n_tiles = bsz * tiles_per_seq
    halo_blocks = t // HALO
    last_halo_block = seq_len // HALO - 1
    est = (4 * t * d * 4 + 2 * ext * d * 2
           + (win.size + pw_bd.size + wout.size + wgu.size + wd.size) * 2
           + ext * d * 2 + ext * (pool_width + conv_width) * 4 + t * pool_width * 2 + 2 * t * conv_width * 4
           + t * (pool_width + conv_width) * 2
           + t * d * 4 + t * d * 2 + t * wd.shape[0] * 2
           + ext * win.shape[1] * 4 + 2 * t * d * 4 + 3 * t * 2 * FFN_CHUNK * 4)

    def stage_tile(g, lag):
        tile = jnp.clip(g - lag, 0, n_tiles - 1)
        return tile // tiles_per_seq, tile % tiles_per_seq

    def stage_block(lag):
        def index_map(g):
            b, i = stage_tile(g, lag)
            return b, i, 0
        return index_map

    def left_halo_block(g):
        b, i = stage_tile(g, 0)
        return b, jnp.maximum(i * halo_blocks - 1, 0), 0

    def right_halo_block(g):
        b, i = stage_tile(g, 0)
        return b, jnp.minimum((i + 1) * halo_blocks, last_halo_block), 0

    kern = functools.partial(_mixer_ffn_kernel, n_tiles=n_tiles, tiles_per_seq=tiles_per_seq, seq_len=seq_len)
    return pl.pallas_call(
        kern,
        grid=(n_tiles + 2,),
        in_specs=[
            pl.BlockSpec((1, HALO, d), left_halo_block),
            pl.BlockSpec((1, t, d), stage_block(0)),
            pl.BlockSpec((1, HALO, d), right_halo_block),
            pl.BlockSpec((1, t, d), stage_block(1)),
        ] + [_resident()] * 13,
        out_specs=pl.BlockSpec((1, t, d), stage_block(2)),
        out_shape=jax.ShapeDtypeStruct((bsz, seq_len, d), jnp.float32),
        scratch_shapes=[
            pltpu.VMEM((ext, d), jnp.bfloat16),
            pltpu.VMEM((ext, pool_width), jnp.float32),
            pltpu.VMEM((ext, conv_width), jnp.float32),
            pltpu.VMEM((t, pool_width), jnp.bfloat16),
            pltpu.VMEM((t, conv_width), jnp.float32),
            pltpu.VMEM((t, conv_width), jnp.float32),
            pltpu.VMEM((t, pool_width + conv_width), jnp.bfloat16),
            pltpu.VMEM((t, d), jnp.float32),
            pltpu.VMEM((t, d), jnp.bfloat16),
            pltpu.VMEM((t, wd.shape[0]), jnp.bfloat16),
        ],
        compiler_params=pltpu.CompilerParams(
            dimension_semantics=("arbitrary",), vmem_limit_bytes=_vmem_limit(est)),
        name="mixer_ffn",
    )(h3d, h3d, h3d, x3d, win, pw_bd, pb, ps, dww, dwb, lng, lnb, wout, gain, wgu, wd, fgain)


def _ffn_weights(w_gu, w_down):
    assert w_gu.shape[1] == 2 * w_down.shape[0]
    assert w_down.shape[0] % FFN_CHUNK == 0 and FFN_CHUNK % MXU_COLS_V7X == 0
    return w_gu.astype(jnp.bfloat16), w_down.astype(jnp.bfloat16)


def _block_diag(pool_w):
    g, c, _ = pool_w.shape
    eye = jnp.eye(g, dtype=pool_w.dtype)
    return jnp.einsum("gcd,gh->gchd", pool_w, eye).reshape(g * c, g * c)


def kernel(x_prompt, x_sample, ffn1_norm, ffn1_w_gu, ffn1_w_down, mix_norm, w_in, pool_w, pool_b, pool_scale,
           dw_w, dw_b, conv_ln_g, conv_ln_b, w_out, ffn2_norm, ffn2_w_gu, ffn2_w_down, final_norm):
    assert ffn1_norm.shape[0] == 1, "single-layer block"
    row = lambda v: v.reshape(1, -1)
    ffn1 = (row(ffn1_norm[0]),) + _ffn_weights(ffn1_w_gu[0], ffn1_w_down[0]) + (row(mix_norm[0]),)
    mix = (w_in[0].astype(jnp.bfloat16), _block_diag(pool_w[0]).astype(jnp.bfloat16), row(pool_b[0]),
           row(pool_scale[0]), dw_w[0], row(dw_b[0]), row(conv_ln_g[0]), row(conv_ln_b[0]),
           w_out[0].astype(jnp.bfloat16))
    ffn2 = (row(ffn2_norm[0]),) + _ffn_weights(ffn2_w_gu[0], ffn2_w_down[0]) + (row(final_norm),)

    def trunk(x):
        bsz, seq_len, d = x.shape
        x1, h = _ffn(x.reshape(bsz * seq_len, d), *ffn1)
        return _mixer_ffn(x1.reshape(bsz, seq_len, d), h.reshape(bsz, seq_len, d), *mix, *ffn2)

    return trunk(x_prompt), trunk(x_sample)
```

```python
import functools

import jax
import jax.numpy as jnp
from jax import lax
from jax.experimental import pallas as pl
from jax.experimental.pallas import tpu as pltpu

EPS = 1e-6
POOL_WINDOWS = (2, 4, 8, 16)
CONV_KERNEL = 31
CONV_HALF = CONV_KERNEL // 2

SUBLANES_F32 = 8
SUBLANES_BF16 = 16
LANES = 128
MXU_COLS_V7X = 256
VMEM_BYTES_V7X = 64 * 1024 * 1024

HALO = SUBLANES_BF16
POOL_REACH = max(POOL_WINDOWS) // 2
assert HALO >= CONV_HALF + 1 and POOL_REACH == SUBLANES_F32

TOKENS = 512
FFN_STREAMS = 2
FFN_CHUNK = 256
LATE_HEAD_TASKS = 4
LATE_HEAD_START = 6
ROW_CHUNK = 64


def _vmem_limit(estimate_bytes):
    return int(min(estimate_bytes * 5 // 4 + (8 << 20), VMEM_BYTES_V7X - (6 << 20)))


def _rms(x, gain):
    ms = jnp.mean(x * x, axis=-1, keepdims=True)
    return x * lax.rsqrt(ms + EPS) * gain


def _resident():
    return pl.BlockSpec(memory_space=pltpu.VMEM)


def _shift_rows_up(x, r):
    return x if r == 0 else pltpu.roll(x, x.shape[0] - r, axis=0)


def _shift_rows_down(x, r):
    return x if r == 0 else pltpu.roll(x, r, axis=0)


def _zero_after(x):
    bits = lax.bitcast_convert_type(x, jnp.uint32)
    zero = lax.shift_right_logical(lax.shift_right_logical(bits, jnp.uint32(16)), jnp.uint32(16))
    return lax.bitcast_convert_type(zero, jnp.float32)


def _add_tile(x, tile):
    return x + jnp.tile(tile, (x.shape[0] // tile.shape[0], x.shape[1] // tile.shape[1]))


def _swiglu(h_ref, a_ref, wgu_ref, wd_ref, late_tasks=(), late_start=0):
    f = wd_ref.shape[0]
    chunk = FFN_CHUNK
    n_chunks = f // chunk
    assert late_start + len(late_tasks) <= n_chunks
    for c in range(n_chunks):
        cols = slice(c * chunk, (c + 1) * chunk)
        g = jnp.dot(h_ref[...], wgu_ref[:, cols], preferred_element_type=jnp.float32)
        u = jnp.dot(h_ref[...], wgu_ref[:, f + c * chunk:f + (c + 1) * chunk], preferred_element_type=jnp.float32)
        if late_start <= c < late_start + len(late_tasks):
            late_tasks[c - late_start](_zero_after(g[:SUBLANES_F32, :LANES]))
        a_ref[:, cols] = (g * jax.nn.sigmoid(g) * u).astype(jnp.bfloat16)
    return jnp.dot(a_ref[...], wd_ref[...], preferred_element_type=jnp.float32)


def _ffn_kernel(x_ref, gain_ref, wgu_ref, wd_ref, ngain_ref, o_ref, hn_ref, h_ref, a_ref):
    rows = x_ref.shape[0] // FFN_STREAMS
    groups = [pl.ds(k * rows, rows) for k in range(FFN_STREAMS)]
    for grp in groups:
        h_ref[grp, :] = _rms(x_ref[grp, :], gain_ref[...]).astype(jnp.bfloat16)
    for grp in groups:
        y = x_ref[grp, :] + 0.5 * _swiglu(h_ref.at[grp], a_ref.at[grp], wgu_ref, wd_ref)
        o_ref[grp, :] = y
        hn_ref[grp, :] = _rms(y, ngain_ref[...]).astype(jnp.bfloat16)


def _ffn(x2d, gain, wgu, wd, ngain):
    n, d = x2d.shape
    t = FFN_STREAMS * TOKENS
    assert n % t == 0
    est = (4 * t * d * 4 + 2 * t * d * 2
           + wgu.size * 2 + wd.size * 2
           + t * d * 2 + t * wd.shape[0] * 2 + t * d * 4
           + 3 * t * 2 * FFN_CHUNK * 4)
    tile = pl.BlockSpec((t, d), lambda i: (i, 0))
    return pl.pallas_call(
        _ffn_kernel,
        grid=(n // t,),
        in_specs=[tile, _resident(), _resident(), _resident(), _resident()],
        out_specs=(tile, tile),
        out_shape=(jax.ShapeDtypeStruct((n, d), jnp.float32), jax.ShapeDtypeStruct((n, d), jnp.bfloat16)),
        scratch_shapes=[pltpu.VMEM((t, d), jnp.bfloat16), pltpu.VMEM((t, wd.shape[0]), jnp.bfloat16)],
        compiler_params=pltpu.CompilerParams(
            dimension_semantics=("arbitrary",), vmem_limit_bytes=_vmem_limit(est)),
        name="ffn",
    )(x2d, gain, wgu, wd, ngain)


def _pool_chunk(p_ref, base, cols, window, pos, seq_len, zero):
    r = ROW_CHUNK
    x = _add_tile(p_ref[pl.ds(base + HALO - POOL_REACH, r + 2 * POOL_REACH), cols], zero)
    s = x + _shift_rows_down(x, 1)
    w = 2
    while w < window:
        s = _shift_rows_down(s, w // 2) + _shift_rows_up(s, w // 2)
        w *= 2
    half = window // 2
    cnt = window - jnp.maximum(half - pos, 0) - jnp.maximum(pos + half - seq_len, 0)
    mid = slice(POOL_REACH, POOL_REACH + r)
    return s[mid] / cnt.astype(jnp.float32) - x[mid]


def _conv_chunk(v_ref, dww_ref, dwb_ref, base, cols, zero):
    r = ROW_CHUNK
    v = _add_tile(v_ref[pl.ds(base, r + 2 * HALO), cols], zero)
    y = None
    for j in range(SUBLANES_F32):
        rows = r if j == 0 else r + SUBLANES_F32
        z = None
        for q in range((CONV_KERNEL + 1) // SUBLANES_F32):
            s = SUBLANES_F32 * q + j
            if s == 0:
                continue
            term = v[SUBLANES_F32 * q:SUBLANES_F32 * q + rows] * dww_ref[s - 1:s, cols]
            z = term if z is None else z + term
        z = _shift_rows_up(z, j)[:r]
        y = z if y is None else y + z
    return y + dwb_ref[:, cols]


def _mixer_project(i, hl_ref, hm_ref, hr_ref, win_ref, h_ref, p_ref, v_ref, *, seq_len):
    t = hm_ref.shape[1]
    ext = t + 2 * HALO
    pool_width = p_ref.shape[1]
    conv_width = v_ref.shape[1]
    h_ref[0:HALO, :] = hl_ref[0]
    h_ref[HALO:HALO + t, :] = hm_ref[0]
    h_ref[HALO + t:ext, :] = hr_ref[0]

    pos_ext = i * t - HALO + lax.broadcasted_iota(jnp.int32, (ext, 1), 0)
    inside = (pos_ext >= 0) & (pos_ext < seq_len)
    h = h_ref[...]
    u_pool = jnp.dot(h, win_ref[:, :pool_width], preferred_element_type=jnp.float32)
    p_ref[...] = jnp.where(inside, u_pool, 0.0)
    u_val = jnp.dot(h, win_ref[:, pool_width:pool_width + conv_width], preferred_element_type=jnp.float32)
    u_gate = jnp.dot(h, win_ref[:, pool_width + conv_width:], preferred_element_type=jnp.float32)
    v = jnp.where(inside, u_val * jax.nn.sigmoid(u_gate), 0.0)
    v_ref[...] = v
    return _zero_after(v[:SUBLANES_F32, :LANES]) + _zero_after(v[-SUBLANES_F32:, -LANES:])


def _mixer_window_tasks(i, p_ref, v_ref, dww_ref, dwb_ref, pooled_ref, y_ref, *, seq_len):
    t, pool_width = pooled_ref.shape
    conv_width = y_ref.shape[1]
    group = pool_width // len(POOL_WINDOWS)
    row = lax.broadcasted_iota(jnp.int32, (ROW_CHUNK, LANES), 0)

    def pool_task(base, cols, window, zero):
        pos = row + (i * t + base)
        pooled = _pool_chunk(p_ref, base, cols, window, pos, seq_len, zero)
        pooled_ref[base:base + ROW_CHUNK, cols] = pooled.astype(jnp.bfloat16)

    def conv_task(base, cols, zero):
        y = _conv_chunk(v_ref, dww_ref, dwb_ref, base, cols, zero)
        y_ref[base:base + ROW_CHUNK, cols] = y
        return _zero_after(y[:SUBLANES_F32, :])

    pool_tasks, conv_tasks = [], []
    for base in range(0, t, ROW_CHUNK):
        for gi, window in enumerate(POOL_WINDOWS):
            for c0 in range(gi * group, (gi + 1) * group, LANES):
                pool_tasks.append(functools.partial(pool_task, base, slice(c0, c0 + LANES), window))
        for c0 in range(0, conv_width, LANES):
            conv_tasks.append(functools.partial(conv_task, base, slice(c0, c0 + LANES)))
    return pool_tasks, conv_tasks


def _pool_head(pooled_ref, pw_ref, pb_ref, ps_ref, cat_ref):
    pool_width = pooled_ref.shape[1]
    for c0 in range(0, pool_width, MXU_COLS_V7X):
        cols = slice(c0, c0 + MXU_COLS_V7X)
        a = jnp.dot(pooled_ref[:, cols], pw_ref[cols, cols], preferred_element_type=jnp.float32)
        cat_ref[:, cols] = ((a + pb_ref[:, cols]) * ps_ref[:, cols]).astype(jnp.bfloat16)


def _conv_head_tasks(y_ref, lng_ref, lnb_ref, cat_ref, n_tasks):
    t, conv_width = y_ref.shape
    pool_width = cat_ref.shape[1] - conv_width
    rows = t // n_tasks

    def task(r0, zero):
        y = _add_tile(y_ref[r0:r0 + rows, :], zero)
        mu = jnp.mean(y, axis=-1, keepdims=True)
        yc = y - mu
        var = jnp.mean(yc * yc, axis=-1, keepdims=True)
        ln = yc * lax.rsqrt(var + EPS) * lng_ref[...] + lnb_ref[...]
        cat_ref[r0:r0 + rows, pool_width:] = (ln * jax.nn.sigmoid(ln)).astype(jnp.bfloat16)

    return [functools.partial(task, r0) for r0 in range(0, t, rows)]


def _mixer_ffn_kernel(hl_ref, hm_ref, hr_ref, x_ref, win_ref, pw_ref, pb_ref, ps_ref, dww_ref, dwb_ref,
                      lng_ref, lnb_ref, wout_ref, gain_ref, wgu_ref, wd_ref, fgain_ref, o_ref,
                      h_ref, p_ref, v_ref, pooled_ref, y_ref, yprev_ref, cat_ref, x2_ref, hf_ref, a_ref,
                      *, n_tiles, tiles_per_seq, seq_len):
    g = pl.program_id(0)

    @pl.when(g == 0)
    def _():
        pooled_ref[...] = jnp.zeros_like(pooled_ref)
        y_ref[...] = jnp.zeros_like(y_ref)
        x2_ref[...] = jnp.zeros_like(x2_ref)

    x2 = x2_ref[...]
    hf = _rms(x2, gain_ref[...])
    hf_ref[...] = hf.astype(jnp.bfloat16)
    o_ref[0] = x2
    ffn_input_done = _zero_after(hf[:SUBLANES_F32, :LANES]) + _zero_after(hf[-SUBLANES_F32:, -LANES:])

    _pool_head(pooled_ref, pw_ref, pb_ref, ps_ref, cat_ref)
    yprev_ref[...] = y_ref[...]
    conv_head = _conv_head_tasks(yprev_ref, lng_ref, lnb_ref, cat_ref, LATE_HEAD_TASKS)

    i = lax.rem(jnp.minimum(g, n_tiles - 1), tiles_per_seq)
    glu_done = _mixer_project(i, hl_ref, hm_ref, hr_ref, win_ref, h_ref, p_ref, v_ref, seq_len=seq_len)
    pool_tasks, conv_tasks = _mixer_window_tasks(i, p_ref, v_ref, dww_ref, dwb_ref, pooled_ref, y_ref, seq_len=seq_len)
    assert len(pool_tasks) == len(conv_tasks)
    conv_done = [task(glu_done + ffn_input_done) for task in conv_tasks]
    for task, after in zip(pool_tasks, conv_done):
        task(after)
    acc = _swiglu(hf_ref, a_ref, wgu_ref, wd_ref, late_tasks=conv_head, late_start=LATE_HEAD_START)
    x2_ref[...] = x_ref[0] + jnp.dot(cat_ref[...], wout_ref[...], preferred_element_type=jnp.float32)
    o_ref[0] = _rms(o_ref[0] + 0.5 * acc, fgain_ref[...])


def _mixer_ffn(x3d, h3d, win, pw_bd, pb, ps, dww, dwb, lng, lnb, wout, gain, wgu, wd, fgain):
    bsz, seq_len, d = x3d.shape
    t = TOKENS
    assert seq_len % t == 0 and t % HALO == 0 and t % ROW_CHUNK == 0 and t % LATE_HEAD_TASKS == 0
    ext = t + 2 * HALO
    pool_width = pw_bd.shape[0]
    conv_width = dww.shape[1]
    assert pool_width % (len(POOL_WINDOWS) * LANES) == 0 and conv_width % LANES == 0
    assert pool_width % MXU_COLS_V7X == 0 and MXU_COLS_V7X % (pool_width // len(POOL_WINDOWS)) == 0
    tiles_per_seq = seq_len // t
    n_tiles = bsz * tiles_per_seq
    halo_blocks = t // HALO
    last_halo_block = seq_len // HALO - 1
    est = (4 * t * d * 4 + 2 * ext * d * 2
           + (win.size + pw_bd.size + wout.size + wgu.size + wd.size) * 2
           + ext * d * 2 + ext * (pool_width + conv_width) * 4 + t * pool_width * 2 + 2 * t * conv_width * 4
           + t * (pool_width + conv_width) * 2
           + t * d * 4 + t * d * 2 + t * wd.shape[0] * 2
           + ext * win.shape[1] * 4 + 2 * t * d * 4 + 3 * t * 2 * FFN_CHUNK * 4)

    def stage_tile(g, lag):
        tile = jnp.clip(g - lag, 0, n_tiles - 1)
        return tile // tiles_per_seq, tile % tiles_per_seq

    def stage_block(lag):
        def index_map(g):
            b, i = stage_tile(g, lag)
            return b, i, 0
        return index_map

    def left_halo_block(g):
        b, i = stage_tile(g, 0)
        return b, jnp.maximum(i * halo_blocks - 1, 0), 0

    def right_halo_block(g):
        b, i = stage_tile(g, 0)
        return b, jnp.minimum((i + 1) * halo_blocks, last_halo_block), 0

    kern = functools.partial(_mixer_ffn_kernel, n_tiles=n_tiles, tiles_per_seq=tiles_per_seq, seq_len=seq_len)
    return pl.pallas_call(
        kern,
        grid=(n_tiles + 2,),
        in_specs=[
            pl.BlockSpec((1, HALO, d), left_halo_block),
            pl.BlockSpec((1, t, d), stage_block(0)),
            pl.BlockSpec((1, HALO, d), right_halo_block),
            pl.BlockSpec((1, t, d), stage_block(1)),
        ] + [_resident()] * 13,
        out_specs=pl.BlockSpec((1, t, d), stage_block(2)),
        out_shape=jax.ShapeDtypeStruct((bsz, seq_len, d), jnp.float32),
        scratch_shapes=[
            pltpu.VMEM((ext, d), jnp.bfloat16),
            pltpu.VMEM((ext, pool_width), jnp.float32),
            pltpu.VMEM((ext, conv_width), jnp.float32),
            pltpu.VMEM((t, pool_width), jnp.bfloat16),
            pltpu.VMEM((t, conv_width), jnp.float32),
            pltpu.VMEM((t, conv_width), jnp.float32),
            pltpu.VMEM((t, pool_width + conv_width), jnp.bfloat16),
            pltpu.VMEM((t, d), jnp.float32),
            pltpu.VMEM((t, d), jnp.bfloat16),
            pltpu.VMEM((t, wd.shape[0]), jnp.bfloat16),
        ],
        compiler_params=pltpu.CompilerParams(
            dimension_semantics=("arbitrary",), vmem_limit_bytes=_vmem_limit(est)),
        name="mixer_ffn",
    )(h3d, h3d, h3d, x3d, win, pw_bd, pb, ps, dww, dwb, lng, lnb, wout, gain, wgu, wd, fgain)


def _ffn_weights(w_gu, w_down):
    assert w_gu.shape[1] == 2 * w_down.shape[0]
    assert w_down.shape[0] % FFN_CHUNK == 0 and FFN_CHUNK % MXU_COLS_V7X == 0
    return w_gu.astype(jnp.bfloat16), w_down.astype(jnp.bfloat16)


def _block_diag(pool_w):
    g, c, _ = pool_w.shape
    eye = jnp.eye(g, dtype=pool_w.dtype)
    return jnp.einsum("gcd,gh->gchd", pool_w, eye).reshape(g * c, g * c)


def kernel(x_prompt, x_sample, ffn1_norm, ffn1_w_gu, ffn1_w_down, mix_norm, w_in, pool_w, pool_b, pool_scale,
           dw_w, dw_b, conv_ln_g, conv_ln_b, w_out, ffn2_norm, ffn2_w_gu, ffn2_w_down, final_norm):
    assert ffn1_norm.shape[0] == 1, "single-layer block"
    row = lambda v: v.reshape(1, -1)
    ffn1 = (row(ffn1_norm[0]),) + _ffn_weights(ffn1_w_gu[0], ffn1_w_down[0]) + (row(mix_norm[0]),)
    mix = (w_in[0].astype(jnp.bfloat16), _block_diag(pool_w[0]).astype(jnp.bfloat16), row(pool_b[0]),
           row(pool_scale[0]), dw_w[0], row(dw_b[0]), row(conv_ln_g[0]), row(conv_ln_b[0]),
           w_out[0].astype(jnp.bfloat16))
    ffn2 = (row(ffn2_norm[0]),) + _ffn_weights(ffn2_w_gu[0], ffn2_w_down[0]) + (row(final_norm),)

    def trunk(x):
        bsz, seq_len, d = x.shape
        x1, h = _ffn(x.reshape(bsz * seq_len, d), *ffn1)
        return _mixer_ffn(x1.reshape(bsz, seq_len, d), h.reshape(bsz, seq_len, d), *mix, *ffn2)

    return trunk(x_prompt), trunk(x_sample)
```
